```python
import jax, jax.numpy as jnp
from jax import lax
import numpy as np

D_MODEL = 1024
BATCH = 2
SEQ = 16384
DEPTH = 4
DEC_BATCH = 8
DEC_SEQ = 8192
PAST_LEN = 128

HEAD_DIM = 64
A_HEADS = 8
A_KV_HEADS = 2
WINDOW = 128
BLOCK = 128
B_HEADS = 8
B_KV_HEADS = 2
ROPE_BASE = 10000.0
GRID_W = 64
C_WIDTH = 1024
C_BLOCKS = 8
C_BLOCK_W = C_WIDTH // C_BLOCKS
C_CONV = 4
LRU_C = 8.0
N_BRANCH = 3
X_HEADS = 4
X_HEAD_DIM = 128
MEM_LEN = 256
D_FF = 3 * D_MODEL
FFN_CONV = 3
ALPHA = (2.0 * DEPTH) ** 0.25
BETA = (8.0 * DEPTH) ** -0.25
LN_EPS = 1e-5
RMS_EPS = 1e-6
NEG_INF = -1e30

A_Q = A_HEADS * HEAD_DIM
A_KV = A_KV_HEADS * HEAD_DIM
B_Q = B_HEADS * HEAD_DIM
B_KV = B_KV_HEADS * HEAD_DIM
X_W = X_HEADS * X_HEAD_DIM
SPLIT_SIZES = (A_Q, A_KV, A_KV, B_Q, B_KV, B_KV, C_WIDTH, C_WIDTH, N_BRANCH * D_MODEL)
N_IN = sum(SPLIT_SIZES)
SPLIT_POINTS = tuple(int(v) for v in np.cumsum(SPLIT_SIZES)[:-1])

kernel_name = 'hybrid_bidir_encoder'


def layer_norm(x, g, b):
    xf = x.astype(jnp.float32)
    mu = jnp.mean(xf, axis=-1, keepdims=True)
    var = jnp.mean(jnp.square(xf - mu), axis=-1, keepdims=True)
    y = (xf - mu) * lax.rsqrt(var + LN_EPS) * g.astype(jnp.float32) + b.astype(jnp.float32)
    return y.astype(x.dtype)


def rms_norm(x, g):
    xf = x.astype(jnp.float32)
    y = xf * lax.rsqrt(jnp.mean(jnp.square(xf), axis=-1, keepdims=True) + RMS_EPS) * g.astype(jnp.float32)
    return y.astype(x.dtype)


def depthwise_conv(x, w, b):
    K = w.shape[0]
    T = x.shape[1]
    left = (K - 1) // 2
    xp = jnp.pad(x, ((0, 0), (left, K - 1 - left), (0, 0)))
    out = b
    for k in range(K):
        out = out + xp[:, k:k + T] * w[k]
    return out


def alibi_slopes(n_heads):
    return jnp.asarray([2.0 ** (-8.0 * (h + 1) / n_heads) for h in range(n_heads)], dtype=jnp.float32)


def window_attention(q, k, v, sink, slopes):
    Bsz, T, H, hd = q.shape
    KV = k.shape[2]
    G = H // KV
    nb = T // BLOCK
    qb = q.reshape(Bsz, nb, BLOCK, KV, G, hd)

    def neighbours(z):
        zp = jnp.pad(z, ((0, 0), (BLOCK, BLOCK), (0, 0), (0, 0)))
        zb = zp.reshape(Bsz, nb + 2, BLOCK, KV, hd)
        return jnp.concatenate([zb[:, :-2], zb[:, 1:-1], zb[:, 2:]], axis=2)

    kw = neighbours(k)
    vw = neighbours(v)
    s = jnp.einsum('bnqkgd,bnskd->bnkgqs', qb, kw).astype(jnp.float32) * (hd ** -0.5)
    blk = jnp.arange(nb)
    qpos = blk[:, None] * BLOCK + jnp.arange(BLOCK)[None, :]
    kpos = (blk[:, None] - 1) * BLOCK + jnp.arange(3 * BLOCK)[None, :]
    dist = jnp.abs(qpos[:, :, None] - kpos[:, None, :])
    valid = (dist <= WINDOW) & (kpos[:, None, :] >= 0) & (kpos[:, None, :] < T)
    bias = -slopes.reshape(KV, G)[None, None, :, :, None, None] * dist.astype(jnp.float32)[None, :, None, None, :, :]
    s = jnp.where(valid[None, :, None, None], s + bias, NEG_INF)
    sink_l = sink.astype(jnp.float32).reshape(KV, G)[None, None, :, :, None, None]
    m = jnp.maximum(jnp.max(s, axis=-1, keepdims=True), sink_l)
    p = jnp.exp(s - m)
    den = jnp.sum(p, axis=-1, keepdims=True) + jnp.exp(sink_l - m)
    p = (p / den).astype(v.dtype)
    o = jnp.einsum('bnkgqs,bnskd->bnqkgd', p, vw)
    return o.reshape(Bsz, T, H * hd)


def axial_rope_angles(T):
    rows = T // GRID_W
    row = jnp.repeat(jnp.arange(rows, dtype=jnp.float32), GRID_W)
    col = jnp.tile(jnp.arange(GRID_W, dtype=jnp.float32), rows)
    axis_dim = HEAD_DIM // 2
    inv_freq = ROPE_BASE ** (-jnp.arange(0, axis_dim, 2, dtype=jnp.float32) / axis_dim)
    ang = jnp.concatenate([row[:, None] * inv_freq, col[:, None] * inv_freq], axis=-1)
    return jnp.cos(ang), jnp.sin(ang)


def apply_rope(x, cos, sin):
    xf = x.astype(jnp.float32).reshape(x.shape[:-1] + (x.shape[-1] // 2, 2))
    x0 = xf[..., 0]
    x1 = xf[..., 1]
    c = cos[None, :, None, :]
    s = sin[None, :, None, :]
    out = jnp.stack([x0 * c - x1 * s, x0 * s + x1 * c], axis=-1).reshape(x.shape)
    return out.astype(x.dtype)


def dense_block_attention(q, k, v):
    Bsz, T, H, hd = q.shape
    KV = k.shape[2]
    G = H // KV
    nb = T // BLOCK
    qb = q.reshape(Bsz, nb, BLOCK, KV, G, hd).transpose(1, 0, 2, 3, 4, 5)

    def one_block(q_blk):
        s = jnp.einsum('bqkgd,bskd->bkgqs', q_blk, k).astype(jnp.float32) * (hd ** -0.5)
        p = jax.nn.softmax(s, axis=-1).astype(v.dtype)
        return jnp.einsum('bkgqs,bskd->bqkgd', p, v)

    o = lax.map(one_block, qb)
    return o.transpose(1, 0, 2, 3, 4, 5).reshape(Bsz, T, H * hd)


def _lin_rec_combine(e1, e2):
    a1, b1 = e1
    a2, b2 = e2
    return a1 * a2, a2 * b1 + b2


def rg_lru(u, w_r, b_r, w_i, b_i, lam):
    Bsz, T, C = u.shape
    ub = u.reshape(Bsz, T, C_BLOCKS, C_BLOCK_W)
    r = jax.nn.sigmoid(jnp.einsum('btnc,ncd->btnd', ub, w_r.astype(jnp.float32)).reshape(Bsz, T, C) + b_r)
    i = jax.nn.sigmoid(jnp.einsum('btnc,ncd->btnd', ub, w_i.astype(jnp.float32)).reshape(Bsz, T, C) + b_i)
    log_a = -LRU_C * r * jax.nn.softplus(-lam.astype(jnp.float32))
    a = jnp.exp(log_a)
    mult = jnp.sqrt(-jnp.expm1(2.0 * log_a))
    _, h = lax.associative_scan(_lin_rec_combine, (a, mult * i * u), axis=1)
    return h


def token_mixer(x, w_in, sink_a, q_norm_b, k_norm_b, conv_c_w, conv_c_b, w_rec_gate, b_rec_gate,
                w_in_gate, b_in_gate, lru_lambda, w_br_a, w_br_b, w_br_c, w_out):
    Bsz, T, _ = x.shape
    z = x @ w_in
    qa, ka, va, qb, kb, vb, xc, gc, zg = jnp.split(z, SPLIT_POINTS, axis=-1)
    oa = window_attention(qa.reshape(Bsz, T, A_HEADS, HEAD_DIM),
                          ka.reshape(Bsz, T, A_KV_HEADS, HEAD_DIM),
                          va.reshape(Bsz, T, A_KV_HEADS, HEAD_DIM),
                          sink_a, alibi_slopes(A_HEADS))
    cos, sin = axial_rope_angles(T)
    qb = apply_rope(rms_norm(qb.reshape(Bsz, T, B_HEADS, HEAD_DIM), q_norm_b), cos, sin)
    kb = apply_rope(rms_norm(kb.reshape(Bsz, T, B_KV_HEADS, HEAD_DIM), k_norm_b), cos, sin)
    ob = dense_block_attention(qb, kb, vb.reshape(Bsz, T, B_KV_HEADS, HEAD_DIM))
    u = depthwise_conv(xc, conv_c_w, conv_c_b).astype(jnp.float32)
    h_fwd = rg_lru(u, w_rec_gate[0], b_rec_gate[0], w_in_gate[0], b_in_gate[0], lru_lambda[0])
    h_bwd = rg_lru(u[:, ::-1], w_rec_gate[1], b_rec_gate[1], w_in_gate[1], b_in_gate[1], lru_lambda[1])[:, ::-1]
    oc = (h_fwd + h_bwd).astype(x.dtype) * jax.nn.gelu(gc)
    g = jax.nn.sigmoid(zg.reshape(Bsz, T, N_BRANCH, D_MODEL).astype(jnp.float32)).astype(x.dtype)
    merged = g[:, :, 0] * (oa @ w_br_a) + g[:, :, 1] * (ob @ w_br_b) + g[:, :, 2] * (oc @ w_br_c)
    return merged @ w_out


def memory_cross_attention(x, mem, w_cq, w_ckv, w_co):
    Bsz, T, _ = x.shape
    q = (x @ w_cq).reshape(Bsz, T, X_HEADS, X_HEAD_DIM)
    kv = (mem @ w_ckv).reshape(Bsz, mem.shape[1], 2, X_HEADS, X_HEAD_DIM)
    k = kv[:, :, 0]
    v = kv[:, :, 1]
    s = jnp.einsum('bthd,bmhd->bhtm', q, k).astype(jnp.float32) * (X_HEAD_DIM ** -0.5)
    p = jax.nn.softmax(s, axis=-1).astype(v.dtype)
    o = jnp.einsum('bhtm,bmhd->bthd', p, v).reshape(Bsz, T, X_W)
    return o @ w_co


def conv_glu_ffn(x, w_up, conv_f_w, conv_f_b, w_down):
    gate, up = jnp.split(x @ w_up, 2, axis=-1)
    gate = depthwise_conv(gate, conv_f_w, conv_f_b)
    return (jax.nn.gelu(gate) * up) @ w_down


def encoder_layer(x, mem, w_in, sink_a, q_norm_b, k_norm_b, conv_c_w, conv_c_b, w_rec_gate, b_rec_gate,
                  w_in_gate, b_in_gate, lru_lambda, w_br_a, w_br_b, w_br_c, w_out, ln1_g, ln1_b,
                  w_cq, w_ckv, w_co, ln2_g, ln2_b, w_up, conv_f_w, conv_f_b, w_down, ln3_g, ln3_b):
    mix = token_mixer(x, w_in, sink_a, q_norm_b, k_norm_b, conv_c_w, conv_c_b, w_rec_gate, b_rec_gate,
                      w_in_gate, b_in_gate, lru_lambda, w_br_a, w_br_b, w_br_c, w_out)
    x = layer_norm(ALPHA * x + mix, ln1_g, ln1_b)
    x = layer_norm(ALPHA * x + memory_cross_attention(x, mem, w_cq, w_ckv, w_co), ln2_g, ln2_b)
    x = layer_norm(ALPHA * x + conv_glu_ffn(x, w_up, conv_f_w, conv_f_b, w_down), ln3_g, ln3_b)
    return x


def setup_inputs(seed: int = 0) -> dict:
    key = jax.random.key(seed)
    ks = iter(jax.random.split(key, 40))

    def nrm(shape, scale):
        return jax.random.normal(next(ks), shape, jnp.float32) * scale

    L = DEPTH
    a_c = jax.random.uniform(next(ks), (L, 2, C_WIDTH), jnp.float32, 0.81, 0.998)
    s_lam = a_c ** (1.0 / LRU_C)
    return {
        'x_prompt': nrm((BATCH, SEQ, D_MODEL), 1.0),
        'x_sample': nrm((DEC_BATCH, DEC_SEQ, D_MODEL), 1.0),
        'mem_prompt': nrm((BATCH, MEM_LEN, D_MODEL), 1.0),
        'mem_sample': nrm((DEC_BATCH, MEM_LEN, D_MODEL), 1.0),
        'w_in': nrm((L, D_MODEL, N_IN), D_MODEL ** -0.5),
        'sink_a': nrm((L, A_HEADS), 0.5),
        'q_norm_b': 1.0 + nrm((L, HEAD_DIM), 0.1),
        'k_norm_b': 1.0 + nrm((L, HEAD_DIM), 0.1),
        'conv_c_w': nrm((L, C_CONV, C_WIDTH), C_CONV ** -0.5),
        'conv_c_b': nrm((L, C_WIDTH), 0.01),
        'w_rec_gate': nrm((L, 2, C_BLOCKS, C_BLOCK_W, C_BLOCK_W), C_BLOCK_W ** -0.5),
        'b_rec_gate': nrm((L, 2, C_WIDTH), 0.01),
        'w_in_gate': nrm((L, 2, C_BLOCKS, C_BLOCK_W, C_BLOCK_W), C_BLOCK_W ** -0.5),
        'b_in_gate': nrm((L, 2, C_WIDTH), 0.01),
        'lru_lambda': jnp.log(s_lam) - jnp.log1p(-s_lam),
        'w_br_a': nrm((L, A_Q, D_MODEL), A_Q ** -0.5),
        'w_br_b': nrm((L, B_Q, D_MODEL), B_Q ** -0.5),
        'w_br_c': nrm((L, C_WIDTH, D_MODEL), C_WIDTH ** -0.5),
        'w_out': nrm((L, D_MODEL, D_MODEL), BETA * D_MODEL ** -0.5),
        'ln1_g': 1.0 + nrm((L, D_MODEL), 0.05),
        'ln1_b': nrm((L, D_MODEL), 0.02),
        'w_cq': nrm((L, D_MODEL, X_W), D_MODEL ** -0.5),
        'w_ckv': nrm((L, D_MODEL, 2 * X_W), D_MODEL ** -0.5),
        'w_co': nrm((L, X_W, D_MODEL), BETA * X_W ** -0.5),
        'ln2_g': 1.0 + nrm((L, D_MODEL), 0.05),
        'ln2_b': nrm((L, D_MODEL), 0.02),
        'w_up': nrm((L, D_MODEL, 2 * D_FF), D_MODEL ** -0.5),
        'conv_f_w': nrm((L, FFN_CONV, D_FF), FFN_CONV ** -0.5),
        'conv_f_b': nrm((L, D_FF), 0.01),
        'w_down': nrm((L, D_FF, D_MODEL), BETA * D_FF ** -0.5),
        'ln3_g': 1.0 + nrm((L, D_MODEL), 0.05),
        'ln3_b': nrm((L, D_MODEL), 0.02),
    }


def reference(x_prompt, x_sample, mem_prompt, mem_sample, w_in, sink_a, q_norm_b, k_norm_b, conv_c_w, conv_c_b,
              w_rec_gate, b_rec_gate, w_in_gate, b_in_gate, lru_lambda, w_br_a, w_br_b, w_br_c, w_out,
              ln1_g, ln1_b, w_cq, w_ckv, w_co, ln2_g, ln2_b, w_up, conv_f_w, conv_f_b, w_down, ln3_g, ln3_b):
    weights = (w_in, sink_a, q_norm_b, k_norm_b, conv_c_w, conv_c_b, w_rec_gate, b_rec_gate, w_in_gate, b_in_gate,
               lru_lambda, w_br_a, w_br_b, w_br_c, w_out, ln1_g, ln1_b, w_cq, w_ckv, w_co, ln2_g, ln2_b,
               w_up, conv_f_w, conv_f_b, w_down, ln3_g, ln3_b)
    y_prompt = x_prompt
    y_sample = x_sample
    for layer in range(DEPTH):
        wl = [w[layer] for w in weights]
        y_prompt = encoder_layer(y_prompt, mem_prompt, *wl)
        y_sample = encoder_layer(y_sample, mem_sample, *wl)
    return (y_prompt, y_sample)
```

```python
import functools

import jax
import jax.numpy as jnp
import numpy as np
from jax import lax
from jax.experimental import pallas as pl
from jax.experimental.pallas import tpu as pltpu

D_MODEL = 1024
DEPTH = 4
HEAD_DIM = 64
N_HEADS = 8
N_KV = 2
GROUP = N_HEADS // N_KV
WINDOW = 128
ROPE_BASE = 10000.0
GRID_W = 64
C_WIDTH = 1024
C_BLOCKS = 8
C_BLOCK_W = C_WIDTH // C_BLOCKS
C_CONV = 4
LRU_C = 8.0
X_HEADS = 4
X_HEAD_DIM = 128
X_W = X_HEADS * X_HEAD_DIM
D_FF = 3 * D_MODEL
ALPHA = (2.0 * DEPTH) ** 0.25
LN_EPS = 1e-5
RMS_EPS = 1e-6
NEG_INF = -1e30

Q_W = N_HEADS * HEAD_DIM
KV_W = N_KV * HEAD_DIM
KV2_W = 2 * KV_W

LANES = 128
SUBLANES = 8
VMEM_LIMIT = 56 * 1024 * 1024

BF16 = jnp.bfloat16
F32 = jnp.float32


def _params(*sem):
    return pltpu.CompilerParams(dimension_semantics=sem, vmem_limit_bytes=VMEM_LIMIT)


def _const_spec(shape):
    nd = len(shape)
    return pl.BlockSpec(shape, lambda *_: (0,) * nd, pipeline_mode=pl.Buffered(1))


def _layer_norm(y, g, b):
    mu = jnp.mean(y, axis=-1, keepdims=True)
    d = y - mu
    var = jnp.mean(d * d, axis=-1, keepdims=True)
    return d * lax.rsqrt(var + LN_EPS) * g + b


def _dot(a, b):
    return jnp.dot(a, b, preferred_element_type=F32)


def _dot_nt(a, b):
    return lax.dot_general(a, b, (((1,), (1,)), ((), ())), preferred_element_type=F32)


def _group_mean_sq(z, ones_bd):
    sq = z * z
    hi = sq.astype(BF16)
    lo = (sq - hi.astype(F32)).astype(BF16)
    return (_dot(hi, ones_bd) + _dot(lo, ones_bd)) * (1.0 / HEAD_DIM)


def _norm_rope(z, ones_bd, gain, gain_sw, cos, sin, out_ref, scale):
    inv = lax.rsqrt(_group_mean_sq(z, ones_bd) + RMS_EPS)
    n = z * inv
    even = (lax.broadcasted_iota(jnp.int32, (1, LANES), 1) % 2) == 0
    gc = gain * cos * scale
    gs = gain_sw * sin * scale
    for c in range(z.shape[1] // LANES):
        nc = n[:, c * LANES:(c + 1) * LANES]
        partner = jnp.where(even, pltpu.roll(nc, LANES - 1, axis=1), pltpu.roll(nc, 1, axis=1))
        out_ref[:, c * LANES:(c + 1) * LANES] = (nc * gc + partner * gs).astype(out_ref.dtype)


def _inproj_kernel(x_ref, w1_ref, w2_ref, w3_ref, cos_ref, sin_ref, gq_ref, gqs_ref, gk_ref, gks_ref,
                   bdq_ref, bdk_ref,
                   qa_ref, ka_ref, va_ref, qb_ref, kb_ref, vb_ref, xc_ref, gg_ref, g_ref):
    xb = x_ref[...].astype(BF16)
    z1 = _dot(xb, w1_ref[...])
    o = 0
    qa_ref[...] = z1[:, o:o + Q_W].astype(BF16); o += Q_W
    ka_ref[...] = z1[:, o:o + KV2_W].astype(BF16); o += KV2_W
    va_ref[...] = z1[:, o:o + KV2_W].astype(BF16); o += KV2_W
    cos = cos_ref[...]
    sin = sin_ref[...]
    _norm_rope(z1[:, o:o + Q_W], bdq_ref[...], gq_ref[...], gqs_ref[...], cos, sin, qb_ref, HEAD_DIM ** -0.5)
    o += Q_W
    _norm_rope(z1[:, o:o + KV2_W], bdk_ref[...], gk_ref[...], gks_ref[...], cos, sin, kb_ref, 1.0)
    o += KV2_W
    first = (lax.broadcasted_iota(jnp.int32, (1, KV2_W), 1) % LANES) < HEAD_DIM
    vb_ref[...] = jnp.where(first, z1[:, o:o + KV2_W], 1.0).astype(BF16)
    z2 = _dot(xb, w2_ref[...])
    xc_ref[...] = z2[:, :C_WIDTH]
    gg_ref[...] = jax.nn.gelu(z2[:, C_WIDTH:], approximate=True)
    g_ref[...] = jax.nn.sigmoid(_dot(xb, w3_ref[...]))


def _inproj(x, w1, w2, w3, cos, sin, gq, gqs, gk, gks, bdq, bdk, *, tm):
    B, T, _ = x.shape
    nt = T // tm
    n1 = w1.shape[1]
    tok = lambda w: pl.BlockSpec((None, tm, w), lambda b, i: (b, i, 0))
    pos = pl.BlockSpec((tm, LANES), lambda b, i: (i, 0))
    row = _const_spec((1, LANES))
    outs = [(Q_W, BF16), (KV2_W, BF16), (KV2_W, BF16), (Q_W, BF16), (KV2_W, BF16), (KV2_W, BF16),
            (C_WIDTH, F32), (C_WIDTH, F32), (3 * D_MODEL, F32)]
    return pl.pallas_call(
        _inproj_kernel,
        grid=(B, nt),
        in_specs=[tok(D_MODEL), _const_spec((D_MODEL, n1)), _const_spec((D_MODEL, 2 * C_WIDTH)),
                  _const_spec((D_MODEL, 3 * D_MODEL)), pos, pos, row, row, row, row,
                  _const_spec((Q_W, Q_W)), _const_spec((KV2_W, KV2_W))],
        out_specs=[tok(w) for w, _ in outs],
        out_shape=[jax.ShapeDtypeStruct((B, T, w), dt) for w, dt in outs],
        compiler_params=_params("parallel", "parallel"),
        name="inproj",
    )(x, w1, w2, w3, cos, sin, gq, gqs, gk, gks, bdq, bdk)


def _window_kernel(sink_ref, slope_ref, q_ref, kp_ref, kc_ref, kn_ref, vp_ref, vc_ref, vn_ref, o_ref, *, tq, T):
    i = pl.program_id(1)
    nk = tq + 2 * WINDOW
    kcat = jnp.concatenate([kp_ref[...], kc_ref[...], kn_ref[...]], axis=0)
    vcat = jnp.concatenate([vp_ref[...], vc_ref[...], vn_ref[...]], axis=0)
    r = lax.broadcasted_iota(jnp.int32, (tq, nk), 0)
    j = lax.broadcasted_iota(jnp.int32, (tq, nk), 1)
    dist_i = jnp.abs(r + WINDOW - j)
    kpos = i * tq - WINDOW + j
    valid = (dist_i <= WINDOW) & (kpos >= 0) & (kpos < T)
    dist = dist_i.astype(F32)
    low = lax.broadcasted_iota(jnp.int32, (1, LANES), 1) < HEAD_DIM
    for kv in range(N_KV):
        k2 = kcat[:, kv * LANES:(kv + 1) * LANES]
        v2 = vcat[:, kv * LANES:(kv + 1) * LANES]
        for c in range(GROUP // 2):
            col = (kv * (GROUP // 2) + c) * LANES
            qch = q_ref[:, col:col + LANES]
            halves = []
            for e in range(2):
                h = kv * GROUP + 2 * c + e
                qm = jnp.where(low if e == 0 else ~low, qch, jnp.zeros_like(qch))
                s = _dot_nt(qm, k2)
                s = jnp.where(valid, s - slope_ref[h] * dist, NEG_INF)
                sink = sink_ref[h]
                m = jnp.maximum(jnp.max(s, axis=-1, keepdims=True), sink)
                p = jnp.exp(s - m)
                den = jnp.sum(p, axis=-1, keepdims=True) + jnp.exp(sink - m)
                halves.append(_dot((p / den).astype(BF16), v2))
            o_ref[:, col:col + LANES] = jnp.where(low, halves[0], halves[1]).astype(o_ref.dtype)


def _window_attention(sink, slopes, qa, ka2, va2, *, tq):
    B, T, _ = qa.shape
    nt = T // tq
    per = tq // WINDOW
    nw = T // WINDOW
    smem = pl.BlockSpec(memory_space=pltpu.SMEM)
    prev = pl.BlockSpec((None, WINDOW, KV2_W), lambda b, i: (b, jnp.maximum(i * per - 1, 0), 0))
    cur = pl.BlockSpec((None, tq, KV2_W), lambda b, i: (b, i, 0))
    nxt = pl.BlockSpec((None, WINDOW, KV2_W), lambda b, i: (b, jnp.minimum((i + 1) * per, nw - 1), 0))
    qo = pl.BlockSpec((None, tq, Q_W), lambda b, i: (b, i, 0))
    return pl.pallas_call(
        functools.partial(_window_kernel, tq=tq, T=T),
        grid=(B, nt),
        in_specs=[smem, smem, qo, prev, cur, nxt, prev, cur, nxt],
        out_specs=qo,
        out_shape=jax.ShapeDtypeStruct((B, T, Q_W), BF16),
        compiler_params=_params("parallel", "parallel"),
        name="window_attn",
    )(sink, slopes, qa, ka2, ka2, ka2, va2, va2, va2)


def _dense_kernel(q_ref, k_ref, v_ref, o_ref, qm_sc, m_sc, acc_sc, *, tq):
    j = pl.program_id(3)
    low = lax.broadcasted_iota(jnp.int32, (1, LANES), 1) < HEAD_DIM

    @pl.when(j == 0)
    def _():
        for hh in range(GROUP):
            qch = q_ref[:, (hh // 2) * LANES:(hh // 2 + 1) * LANES]
            qm_sc[hh * tq:(hh + 1) * tq, :] = jnp.where(low if hh % 2 == 0 else ~low, qch, jnp.zeros_like(qch))
        m_sc[...] = jnp.full(m_sc.shape, NEG_INF, F32)
        acc_sc[...] = jnp.zeros(acc_sc.shape, F32)

    s = _dot_nt(qm_sc[...], k_ref[...])
    m_prev = m_sc[...]
    m_new = jnp.maximum(m_prev, jnp.max(s, axis=-1, keepdims=True))
    p = jnp.exp(s - m_new).astype(BF16)
    acc_sc[...] = jnp.exp(m_prev - m_new) * acc_sc[...] + _dot(p, v_ref[...])
    m_sc[...] = m_new

    @pl.when(j == pl.num_programs(3) - 1)
    def _():
        for c in range(GROUP // 2):
            a0 = acc_sc[(2 * c) * tq:(2 * c + 1) * tq, :]
            a1 = acc_sc[(2 * c + 1) * tq:(2 * c + 2) * tq, :]
            r0 = pltpu.roll(a0, HEAD_DIM, axis=1)
            r1 = pltpu.roll(a1, HEAD_DIM, axis=1)
            o_ref[:, c * LANES:(c + 1) * LANES] = jnp.where(low, a0 / r0, r1 / a1).astype(o_ref.dtype)


def _dense_attention(qb, kb2, vb2, *, tq, tk):
    B, T, _ = qb.shape
    gw = GROUP * HEAD_DIM
    qo = pl.BlockSpec((None, tq, gw), lambda b, g, i, j: (b, i, g))
    kv = pl.BlockSpec((None, tk, LANES), lambda b, g, i, j: (b, j, g))
    return pl.pallas_call(
        functools.partial(_dense_kernel, tq=tq),
        grid=(B, N_KV, T // tq, T // tk),
        in_specs=[qo, kv, kv],
        out_specs=qo,
        out_shape=jax.ShapeDtypeStruct((B, T, Q_W), BF16),
        scratch_shapes=[pltpu.VMEM((GROUP * tq, LANES), BF16),
                        pltpu.VMEM((GROUP * tq, 1), F32),
                        pltpu.VMEM((GROUP * tq, LANES), F32)],
        compiler_params=_params("parallel", "parallel", "parallel", "arbitrary"),
        name="dense_attn",
    )(qb, kb2, vb2)


def _one_minus_exp(y, exp_y):
    series = -y * (1.0 + y * (1 / 2 + y * (1 / 6 + y * (1 / 24 + y * (1 / 120 + y * (1 / 720 + y * (1 / 5040)))))))
    return jnp.where(y > -0.35, series, 1.0 - exp_y)


def _lru_kernel(x_ref, xp_ref, xn_ref, cw_ref, cb_ref, wg_ref, br_ref, bi_ref, lam_ref, h_ref,
                a_sc, b_sc, carry_sc, *, tt, reverse):
    i = pl.program_id(1)
    nt = pl.num_programs(1)
    blk = (nt - 1 - i) if reverse else i
    x = x_ref[...]
    row = lax.broadcasted_iota(jnp.int32, (tt, 1), 0)
    prev = jnp.where(blk == 0, 0.0, xp_ref[SUBLANES - 1:SUBLANES, :])
    nxt0 = jnp.where(blk == nt - 1, 0.0, xn_ref[0:1, :])
    nxt1 = jnp.where(blk == nt - 1, 0.0, xn_ref[1:2, :])
    x_m1 = jnp.where(row == 0, prev, pltpu.roll(x, 1, axis=0))
    x_p1 = jnp.where(row == tt - 1, nxt0, pltpu.roll(x, tt - 1, axis=0))
    x_p2 = jnp.where(row == tt - 1, nxt1, jnp.where(row == tt - 2, nxt0, pltpu.roll(x, tt - 2, axis=0)))
    cw = cw_ref[...]
    u = cb_ref[...] + x_m1 * cw[0:1] + x * cw[1:2] + x_p1 * cw[2:3] + x_p2 * cw[3:4]
    sp = jax.nn.softplus(-lam_ref[...])
    for n in range(C_BLOCKS):
        sl = slice(n * C_BLOCK_W, (n + 1) * C_BLOCK_W)
        un = u[:, sl]
        ri = _dot(un.astype(BF16), wg_ref[n])
        r = jax.nn.sigmoid(ri[:, :C_BLOCK_W] + br_ref[:, sl])
        ig = jax.nn.sigmoid(ri[:, C_BLOCK_W:] + bi_ref[:, sl])
        log_a = -LRU_C * r * sp[:, sl]
        a = jnp.exp(log_a)
        a_sc[:, sl] = a
        b_sc[:, sl] = jnp.sqrt(_one_minus_exp(2.0 * log_a, a * a)) * ig * un

    @pl.when(i == 0)
    def _():
        carry_sc[...] = jnp.zeros(carry_sc.shape, F32)

    srow = lax.broadcasted_iota(jnp.int32, (SUBLANES, 1), 0)
    ngroups = tt // SUBLANES

    def group(g, carry):
        gi = (ngroups - 1 - g) if reverse else g
        base = pl.multiple_of(gi * SUBLANES, SUBLANES)
        a = a_sc[pl.ds(base, SUBLANES), :]
        b = b_sc[pl.ds(base, SUBLANES), :]
        for d in (1, 2, 4):
            if reverse:
                keep = srow < SUBLANES - d
                shift = SUBLANES - d
            else:
                keep = srow >= d
                shift = d
            b = b + a * jnp.where(keep, pltpu.roll(b, shift, axis=0), 0.0)
            a = a * jnp.where(keep, pltpu.roll(a, shift, axis=0), 1.0)
        h = b + a * carry
        h_ref[pl.ds(base, SUBLANES), :] = h
        last = h[0:1, :] if reverse else h[SUBLANES - 1:SUBLANES, :]
        return jnp.broadcast_to(last, (SUBLANES, C_WIDTH))

    carry_sc[...] = lax.fori_loop(0, ngroups, group, carry_sc[...], unroll=2)


def _rg_lru(xc, cw, cb, wg, br, bi, lam, *, tt, reverse):
    B, T, _ = xc.shape
    nt = T // tt
    per = tt // SUBLANES
    n8 = T // SUBLANES
    blk = (lambda i: nt - 1 - i) if reverse else (lambda i: i)
    main = pl.BlockSpec((None, tt, C_WIDTH), lambda b, i: (b, blk(i), 0))
    prev = pl.BlockSpec((None, SUBLANES, C_WIDTH), lambda b, i: (b, jnp.maximum(blk(i) * per - 1, 0), 0))
    nxt = pl.BlockSpec((None, SUBLANES, C_WIDTH), lambda b, i: (b, jnp.minimum((blk(i) + 1) * per, n8 - 1), 0))
    row = _const_spec((1, C_WIDTH))
    return pl.pallas_call(
        functools.partial(_lru_kernel, tt=tt, reverse=reverse),
        grid=(B, nt),
        in_specs=[main, prev, nxt, _const_spec((C_CONV, C_WIDTH)), row,
                  _const_spec((C_BLOCKS, C_BLOCK_W, 2 * C_BLOCK_W)), row, row, row],
        out_specs=main,
        out_shape=jax.ShapeDtypeStruct((B, T, C_WIDTH), F32),
        scratch_shapes=[pltpu.VMEM((tt, C_WIDTH), F32), pltpu.VMEM((tt, C_WIDTH), F32),
                        pltpu.VMEM((SUBLANES, C_WIDTH), F32)],
        compiler_params=_params("parallel", "arbitrary"),
        name="rg_lru_bwd" if reverse else "rg_lru_fwd",
    )(xc, xc, xc, cw, cb, wg, br, bi, lam)


def _memkv_kernel(mem_ref, w_ref, k_ref, v_ref):
    kv = _dot(mem_ref[...].astype(BF16), w_ref[...])
    k_ref[...] = kv[:, :X_W].astype(BF16)
    v_ref[...] = kv[:, X_W:].astype(BF16)


def _memory_kv(mem, w_ckv):
    B, M, _ = mem.shape
    out = pl.BlockSpec((None, M, X_W), lambda b: (b, 0, 0))
    return pl.pallas_call(
        _memkv_kernel,
        grid=(B,),
        in_specs=[pl.BlockSpec((None, M, D_MODEL), lambda b: (b, 0, 0)), _const_spec((D_MODEL, 2 * X_W))],
        out_specs=[out, out],
        out_shape=[jax.ShapeDtypeStruct((B, M, X_W), BF16)] * 2,
        compiler_params=_params("parallel"),
        name="memory_kv",
    )(mem, w_ckv)


def _post_kernel(x_ref, oa_ref, ob_ref, hf_ref, hb_ref, gg_ref, g_ref, km_ref, vm_ref,
                 wa_ref, wb_ref, wc_ref, wo_ref, wcq_ref, wco_ref,
                 g1_ref, b1_ref, g2_ref, b2_ref, y_ref):
    ma = _dot(oa_ref[...], wa_ref[...]) * g_ref[:, :D_MODEL]
    mb = _dot(ob_ref[...], wb_ref[...]) * g_ref[:, D_MODEL:2 * D_MODEL]
    oc = ((hf_ref[...] + hb_ref[...]) * gg_ref[...]).astype(BF16)
    mc = _dot(oc, wc_ref[...]) * g_ref[:, 2 * D_MODEL:]
    mix = _dot((ma + mb + mc).astype(BF16), wo_ref[...])
    x1 = _layer_norm(ALPHA * x_ref[...] + mix, g1_ref[...], b1_ref[...])
    q = _dot(x1.astype(BF16), wcq_ref[...]).astype(BF16)
    heads = []
    for h in range(X_HEADS):
        sl = slice(h * X_HEAD_DIM, (h + 1) * X_HEAD_DIM)
        s = _dot_nt(q[:, sl], km_ref[:, sl]) * (X_HEAD_DIM ** -0.5)
        e = jnp.exp(s - jnp.max(s, axis=-1, keepdims=True))
        p = e / jnp.sum(e, axis=-1, keepdims=True)
        heads.append(_dot(p.astype(BF16), vm_ref[:, sl]))
    o = jnp.concatenate(heads, axis=-1).astype(BF16)
    x2 = _layer_norm(ALPHA * x1 + _dot(o, wco_ref[...]), g2_ref[...], b2_ref[...])
    y_ref[...] = x2


def _post_mixer(x, oa, ob, hf, hb, gg, g, km, vm, wa, wb, wc, wo, wcq, wco, g1, b1, g2, b2, *, tm):
    B, T, _ = x.shape
    M = km.shape[1]
    tok = lambda w: pl.BlockSpec((None, tm, w), lambda b, i: (b, i, 0))
    mem = pl.BlockSpec((None, M, X_W), lambda b, i: (b, 0, 0))
    row = _const_spec((1, D_MODEL))
    return pl.pallas_call(
        _post_kernel,
        grid=(B, T // tm),
        in_specs=[tok(D_MODEL), tok(Q_W), tok(Q_W), tok(C_WIDTH), tok(C_WIDTH), tok(C_WIDTH), tok(3 * D_MODEL),
                  mem, mem,
                  _const_spec((Q_W, D_MODEL)), _const_spec((Q_W, D_MODEL)), _const_spec((C_WIDTH, D_MODEL)),
                  _const_spec((D_MODEL, D_MODEL)), _const_spec((D_MODEL, X_W)), _const_spec((X_W, D_MODEL)),
                  row, row, row, row],
        out_specs=tok(D_MODEL),
        out_shape=jax.ShapeDtypeStruct((B, T, D_MODEL), F32),
        compiler_params=_params("parallel", "parallel"),
        name="post_mixer",
    )(x, oa, ob, hf, hb, gg, g, km, vm, wa, wb, wc, wo, wcq, wco, g1, b1, g2, b2)


def _ffn_kernel(x_ref, xp_ref, xn_ref, wg_ref, wu_ref, cw_ref, cb_ref, wd_ref, g3_ref, b3_ref, y_ref,
                *, tm, chunk):
    i = pl.program_id(1)
    x = x_ref[...]
    prev = jnp.where(i == 0, 0.0, xp_ref[...])
    nxt = jnp.where(i == pl.num_programs(1) - 1, 0.0, xn_ref[...])
    ne = tm + 2 * SUBLANES
    xe = jnp.concatenate([prev, x, nxt], axis=0).astype(BF16)
    xb = x.astype(BF16)
    acc = jnp.zeros((tm, D_MODEL), F32)
    for c in range(D_FF // chunk):
        sl = slice(c * chunk, (c + 1) * chunk)
        ge = _dot(xe, wg_ref[:, sl])
        main = slice(SUBLANES, SUBLANES + tm)
        gate = (cb_ref[:, sl]
                + pltpu.roll(ge, 1, axis=0)[main] * cw_ref[0:1, sl]
                + ge[main] * cw_ref[1:2, sl]
                + pltpu.roll(ge, ne - 1, axis=0)[main] * cw_ref[2:3, sl])
        up = _dot(xb, wu_ref[:, sl])
        hid = (jax.nn.gelu(gate, approximate=True) * up).astype(BF16)
        acc = acc + _dot(hid, wd_ref[sl, :])
    y_ref[...] = _layer_norm(ALPHA * x + acc, g3_ref[...], b3_ref[...])


def _ffn(x, wg, wu, cw, cb, wd, g3, b3, *, tm, chunk):
    B, T, _ = x.shape
    per = tm // SUBLANES
    n8 = T // SUBLANES
    main = pl.BlockSpec((None, tm, D_MODEL), lambda b, i: (b, i, 0))
    prev = pl.BlockSpec((None, SUBLANES, D_MODEL), lambda b, i: (b, jnp.maximum(i * per - 1, 0), 0))
    nxt = pl.BlockSpec((None, SUBLANES, D_MODEL), lambda b, i: (b, jnp.minimum((i + 1) * per, n8 - 1), 0))
    row = _const_spec((1, D_MODEL))
    return pl.pallas_call(
        functools.partial(_ffn_kernel, tm=tm, chunk=chunk),
        grid=(B, T // tm),
        in_specs=[main, prev, nxt, _const_spec((D_MODEL, D_FF)), _const_spec((D_MODEL, D_FF)),
                  _const_spec((3, D_FF)), _const_spec((1, D_FF)), _const_spec((D_FF, D_MODEL)), row, row],
        out_specs=main,
        out_shape=jax.ShapeDtypeStruct((B, T, D_MODEL), F32),
        compiler_params=_params("parallel", "parallel"),
        name="conv_glu_ffn",
    )(x, x, x, wg, wu, cw, cb, wd, g3, b3)


def _dup_heads(w):
    h0, h1 = w[:, :HEAD_DIM], w[:, HEAD_DIM:]
    return jnp.concatenate([h0, h0, h1, h1], axis=1)


def _rope_tables(T):
    rows = T // GRID_W
    row = jnp.repeat(jnp.arange(rows, dtype=F32), GRID_W)
    col = jnp.tile(jnp.arange(GRID_W, dtype=F32), rows)
    axis_dim = HEAD_DIM // 2
    inv_freq = ROPE_BASE ** (-jnp.arange(0, axis_dim, 2, dtype=F32) / axis_dim)
    ang = jnp.concatenate([row[:, None] * inv_freq, col[:, None] * inv_freq], axis=-1)
    cos = jnp.repeat(jnp.cos(ang), 2, axis=-1)
    sin = jnp.repeat(jnp.sin(ang), 2, axis=-1)
    sign = jnp.tile(jnp.asarray([-1.0, 1.0], F32), HEAD_DIM // 2)
    reps = LANES // HEAD_DIM
    return jnp.tile(cos, (1, reps)), jnp.tile(sin * sign, (1, reps))


def _swap_pairs(g):
    return g.reshape(-1, 2)[:, ::-1].reshape(-1)


def _block_diag_ones(n):
    idx = np.arange(n) // HEAD_DIM
    return jnp.asarray(idx[:, None] == idx[None, :], BF16)


def _prep_layer(l, w_in, sink_a, q_norm_b, k_norm_b, conv_c_w, conv_c_b, w_rec_gate, b_rec_gate,
                w_in_gate, b_in_gate, lru_lambda, w_br_a, w_br_b, w_br_c, w_out, ln1_g, ln1_b,
                w_cq, w_ckv, w_co, ln2_g, ln2_b, w_up, conv_f_w, conv_f_b, w_down, ln3_g, ln3_b):
    w = w_in[l]
    o = 0
    cols = []
    for width in (Q_W, KV_W, KV_W, Q_W, KV_W, KV_W):
        cols.append(w[:, o:o + width]); o += width
    qa, ka, va, qb, kb, vb = cols
    w1 = jnp.concatenate([qa * (HEAD_DIM ** -0.5), _dup_heads(ka), _dup_heads(va),
                          qb, _dup_heads(kb), _dup_heads(vb)], axis=1).astype(BF16)
    w2 = w[:, o:o + 2 * C_WIDTH].astype(BF16)
    w3 = w[:, o + 2 * C_WIDTH:].astype(BF16)
    reps = LANES // HEAD_DIM
    tile = lambda g: jnp.tile(g, reps).reshape(1, LANES)
    row = lambda v: v.reshape(1, -1)
    lru = []
    for d in range(2):
        wg = jnp.concatenate([w_rec_gate[l, d], w_in_gate[l, d]], axis=-1).astype(BF16)
        lru.append((wg, row(b_rec_gate[l, d]), row(b_in_gate[l, d]), row(lru_lambda[l, d])))
    return dict(
        w1=w1, w2=w2, w3=w3,
        gq=tile(q_norm_b[l]), gqs=tile(_swap_pairs(q_norm_b[l])),
        gk=tile(k_norm_b[l]), gks=tile(_swap_pairs(k_norm_b[l])),
        sink=sink_a[l],
        cw=conv_c_w[l], cb=row(conv_c_b[l]), lru=lru,
        wa=w_br_a[l].astype(BF16), wb=w_br_b[l].astype(BF16), wc=w_br_c[l].astype(BF16),
        wo=w_out[l].astype(BF16), wcq=w_cq[l].astype(BF16), wckv=w_ckv[l].astype(BF16),
        wco=w_co[l].astype(BF16),
        g1=row(ln1_g[l]), b1=row(ln1_b[l]), g2=row(ln2_g[l]), b2=row(ln2_b[l]),
        wg=w_up[l][:, :D_FF].astype(BF16), wu=w_up[l][:, D_FF:].astype(BF16),
        fcw=conv_f_w[l], fcb=row(conv_f_b[l]), wd=w_down[l].astype(BF16),
        g3=row(ln3_g[l]), b3=row(ln3_b[l]),
    )


def _tiles(T):
    return dict(tm_in=min(256, T), tq_win=min(128, T), tq=min(256, T), tk=min(1024, T),
                tt=min(256, T), tm_post=min(256, T), tm_ffn=min(256, T), chunk=1024)


def _encoder_layer(x, mem, p, consts):
    T = x.shape[1]
    t = _tiles(T)
    cos, sin, slopes, bdq, bdk = consts
    qa, ka2, va2, qb, kb2, vb2, xc, gg, g = _inproj(
        x, p["w1"], p["w2"], p["w3"], cos, sin, p["gq"], p["gqs"], p["gk"], p["gks"], bdq, bdk, tm=t["tm_in"])
    oa = _window_attention(p["sink"], slopes, qa, ka2, va2, tq=t["tq_win"])
    ob = _dense_attention(qb, kb2, vb2, tq=t["tq"], tk=t["tk"])
    hf = _rg_lru(xc, p["cw"], p["cb"], *p["lru"][0], tt=t["tt"], reverse=False)
    hb = _rg_lru(xc, p["cw"], p["cb"], *p["lru"][1], tt=t["tt"], reverse=True)
    km, vm = _memory_kv(mem, p["wckv"])
    x2 = _post_mixer(x, oa, ob, hf, hb, gg, g, km, vm, p["wa"], p["wb"], p["wc"], p["wo"], p["wcq"], p["wco"],
                     p["g1"], p["b1"], p["g2"], p["b2"], tm=t["tm_post"])
    return _ffn(x2, p["wg"], p["wu"], p["fcw"], p["fcb"], p["wd"], p["g3"], p["b3"], tm=t["tm_ffn"], chunk=t["chunk"])


def _consts(T):
    cos, sin = _rope_tables(T)
    slopes = jnp.asarray([2.0 ** (-8.0 * (h + 1) / N_HEADS) for h in range(N_HEADS)], dtype=F32)
    return cos, sin, slopes, _block_diag_ones(Q_W), _block_diag_ones(KV2_W)


def kernel(x_prompt, x_sample, mem_prompt, mem_sample, w_in, sink_a, q_norm_b, k_norm_b, conv_c_w, conv_c_b, w_rec_gate, b_rec_gate, w_in_gate, b_in_gate, lru_lambda, w_br_a, w_br_b, w_br_c, w_out, ln1_g, ln1_b, w_cq, w_ckv, w_co, ln2_g, ln2_b, w_up, conv_f_w, conv_f_b, w_down, ln3_g, ln3_b):
    weights = (w_in, sink_a, q_norm_b, k_norm_b, conv_c_w, conv_c_b, w_rec_gate, b_rec_gate, w_in_gate, b_in_gate,
               lru_lambda, w_br_a, w_br_b, w_br_c, w_out, ln1_g, ln1_b, w_cq, w_ckv, w_co, ln2_g, ln2_b,
               w_up, conv_f_w, conv_f_b, w_down, ln3_g, ln3_b)
    consts_p = _consts(x_prompt.shape[1])
    consts_s = _consts(x_sample.shape[1])
    y_prompt, y_sample = x_prompt, x_sample
    for l in range(w_in.shape[0]):
        p = _prep_layer(l, *weights)
        y_prompt = _encoder_layer(y_prompt, mem_prompt, p, consts_p)
        y_sample = _encoder_layer(y_sample, mem_sample, p, consts_s)
    return (y_prompt, y_sample)
```
